```python
import math
import jax, jax.numpy as jnp
from jax import lax
import numpy as np

D_MODEL = 2048
BATCH = 1
SEQ = 8192
DEPTH = 2
DEC_BATCH = 8
DEC_SEQ = 64
PAST_LEN = 2048

CHUNK = 64
H_MLSTM = 4
DH_MLSTM = 256
W_MLSTM = H_MLSTM * DH_MLSTM
H_ATT = 8
DH_ATT = 128
W_ATT = H_ATT * DH_ATT
W_MIX = W_MLSTM + W_ATT
BAND_CHUNKS = 8
BAND_PAST = BAND_CHUNKS * CHUNK
BAND_LEN = (BAND_CHUNKS + 1) * CHUNK
REL_CLIP = 256
N_REL = 2 * REL_CLIP + 1
N_MEM = 256
H_MEM = 4
DH_MEM = 128
W_MEM = H_MEM * DH_MEM
D_FF = 5632
CONV_W = 3
EPS = 1e-6
NEG = -1e30
F_BIAS_LO = 3.0
F_BIAS_HI = 6.0
IN_SIZES = (W_MLSTM, W_MLSTM, W_MLSTM, H_MLSTM, H_MLSTM, W_MLSTM, W_ATT, W_ATT, W_ATT)
N_IN = sum(IN_SIZES)

kernel_name = "hybrid_mlstm_chunkband_stream_step"


def rmsnorm(x, g):
    x32 = x.astype(jnp.float32)
    y = x32 * lax.rsqrt(jnp.mean(x32 * x32, axis=-1, keepdims=True) + EPS) * g.astype(jnp.float32)
    return y.astype(x.dtype)


def mix_projections(h, w_in, b_i, b_f):
    B, T = h.shape[:2]
    z = h @ w_in
    offs = [int(o) for o in np.cumsum(IN_SIZES)[:-1]]
    qm, km, vm, ig, fg, og, qa, ka, va = jnp.split(z, offs, axis=-1)
    f32 = jnp.float32
    qm = qm.reshape(B, T, H_MLSTM, DH_MLSTM).astype(f32)
    km = km.reshape(B, T, H_MLSTM, DH_MLSTM).astype(f32) * (DH_MLSTM ** -0.5)
    vm = vm.reshape(B, T, H_MLSTM, DH_MLSTM).astype(f32)
    ig = (ig + b_i).astype(f32)
    lf = jax.nn.log_sigmoid((fg + b_f).astype(f32))
    o = jax.nn.sigmoid(og)
    qa = qa.reshape(B, T, H_ATT, DH_ATT)
    ka = ka.reshape(B, T, H_ATT, DH_ATT)
    va = va.reshape(B, T, H_ATT, DH_ATT)
    return (qm, km, vm, ig, lf, o), (qa, ka, va)


def _to_chunks(a, L):
    B, T = a.shape[:2]
    a = a.reshape((B, T // L, L) + a.shape[2:])
    return a.transpose((1, 0, 3, 2) + tuple(range(4, a.ndim)))


def _mlstm_chunk(carry, xs):
    C, n, m = carry
    q, k, v, ig, lf = xs
    L = q.shape[2]
    b = jnp.cumsum(lf, axis=-1)
    causal = jnp.tril(jnp.ones((L, L), dtype=bool))
    Dm = jnp.where(causal, b[..., :, None] - b[..., None, :] + ig[..., None, :], NEG)
    inter = b + m[..., None]
    m_t = jnp.maximum(inter, jnp.max(Dm, axis=-1))
    w_inter = jnp.exp(inter - m_t)
    P = jnp.exp(Dm - m_t[..., None]) * jnp.einsum('bhtd,bhsd->bhts', q, k)
    num = w_inter[..., None] * jnp.einsum('bhvk,bhtk->bhtv', C, q) + jnp.einsum('bhts,bhsv->bhtv', P, v)
    den = w_inter * jnp.einsum('bhk,bhtk->bht', n, q) + jnp.sum(P, axis=-1)
    h = num / jnp.maximum(jnp.abs(den), jnp.exp(-m_t))[..., None]
    b_L = b[..., -1]
    g = b_L[..., None] - b + ig
    m_new = jnp.maximum(b_L + m, jnp.max(g, axis=-1))
    a_old = jnp.exp(b_L + m - m_new)
    a_s = jnp.exp(g - m_new[..., None])
    C_new = a_old[..., None, None] * C + jnp.einsum('bhs,bhsv,bhsk->bhvk', a_s, v, k)
    n_new = a_old[..., None] * n + jnp.einsum('bhs,bhsk->bhk', a_s, k)
    return (C_new, n_new, m_new), h


def mlstm_scan(q, k, v, ig, lf, C0, n0, m0, L):
    B, T = q.shape[:2]
    xs = (_to_chunks(q, L), _to_chunks(k, L), _to_chunks(v, L), _to_chunks(ig, L), _to_chunks(lf, L))
    (C, n, m), h = lax.scan(_mlstm_chunk, (C0, n0, m0), xs)
    h = h.transpose(1, 0, 3, 2, 4).reshape(B, T, H_MLSTM, DH_MLSTM)
    return (C, n, m), h


def mlstm_out(hm, o, g_head):
    B, T = hm.shape[:2]
    hn = hm * lax.rsqrt(jnp.mean(hm * hm, axis=-1, keepdims=True) + EPS) * g_head.astype(jnp.float32)
    return o * hn.reshape(B, T, W_MLSTM).astype(o.dtype)


def band_attention_prompt(q, k, v, rel_bias):
    B, T, H, d = q.shape
    nc = T // CHUNK
    qc = q.reshape(B, nc, CHUNK, H, d)

    def band(a):
        a = a.reshape(B, nc, CHUNK, H, d)
        a = jnp.pad(a, ((0, 0), (BAND_CHUNKS, 0), (0, 0), (0, 0), (0, 0)))
        return jnp.concatenate([a[:, w:w + nc] for w in range(BAND_CHUNKS + 1)], axis=2)

    kb, vb = band(k), band(v)
    key_pos = jnp.arange(nc)[:, None] * CHUNK - BAND_PAST + jnp.arange(BAND_LEN)[None, :]
    valid = key_pos >= 0
    rel = BAND_PAST + jnp.arange(CHUNK)[:, None] - jnp.arange(BAND_LEN)[None, :]
    bias = rel_bias[:, jnp.clip(rel, -REL_CLIP, REL_CLIP) + REL_CLIP].astype(jnp.float32)
    s = jnp.einsum('bcqhd,bckhd->bchqk', qc, kb).astype(jnp.float32) * (d ** -0.5) + bias[None, None]
    s = jnp.where(valid[None, :, None, None, :], s, NEG)
    p = jax.nn.softmax(s, axis=-1).astype(v.dtype)
    o = jnp.einsum('bchqk,bckhd->bcqhd', p, vb)
    return o.reshape(B, T, H * d)


def band_attention_sample(q, k_new, v_new, k_cache, v_cache, rel_bias):
    B, S, H, d = q.shape
    Lc = k_cache.shape[1]
    kk = jnp.concatenate([k_cache.astype(k_new.dtype), k_new], axis=1)
    vv = jnp.concatenate([v_cache.astype(v_new.dtype), v_new], axis=1)
    rel = Lc + jnp.arange(S)[:, None] - jnp.arange(Lc + S)[None, :]
    bias = rel_bias[:, jnp.clip(rel, -REL_CLIP, REL_CLIP) + REL_CLIP].astype(jnp.float32)
    s = jnp.einsum('bqhd,bkhd->bhqk', q, kk).astype(jnp.float32) * (d ** -0.5) + bias[None]
    p = jax.nn.softmax(s, axis=-1).astype(vv.dtype)
    o = jnp.einsum('bhqk,bkhd->bqhd', p, vv)
    return o.reshape(B, S, H * d)


def mem_kv(mem, g, w_ckv):
    B, M = mem.shape[:2]
    z = rmsnorm(mem, g) @ w_ckv
    k, v = jnp.split(z, 2, axis=-1)
    return k.reshape(B, M, H_MEM, DH_MEM), v.reshape(B, M, H_MEM, DH_MEM)


def cross_attn(h, mk, mv, w_cq, w_co):
    B, T = h.shape[:2]
    q = (h @ w_cq).reshape(B, T, H_MEM, DH_MEM)
    s = jnp.einsum('bthd,bmhd->bhtm', q, mk.astype(q.dtype)).astype(jnp.float32) * (DH_MEM ** -0.5)
    p = jax.nn.softmax(s, axis=-1).astype(q.dtype)
    o = jnp.einsum('bhtm,bmhd->bthd', p, mv.astype(q.dtype)).reshape(B, T, W_MEM)
    return o @ w_co


def conv_ffn(h, conv_state, w_up, conv_w, conv_b, w_down):
    T = h.shape[1]
    u = h @ w_up
    ext = jnp.concatenate([conv_state.astype(u.dtype), u], axis=1)
    c = conv_b + sum(conv_w[j] * ext[:, j:j + T] for j in range(CONV_W))
    a, b = jnp.split(c, 2, axis=-1)
    y = (jax.nn.silu(a) * b) @ w_down
    return y, ext[:, -(CONV_W - 1):]


def setup_inputs(seed: int = 0) -> dict:
    key = jax.random.key(seed)
    ks = iter(jax.random.split(key, 40))
    f32 = jnp.float32

    def nrm(shape, scale=1.0):
        return jax.random.normal(next(ks), shape, f32) * scale

    def gain(shape):
        return 1.0 + nrm(shape, 0.01)

    att_rows = min(BAND_PAST, PAST_LEN)
    b_f = jnp.linspace(F_BIAS_LO, F_BIAS_HI, H_MLSTM, dtype=f32)[None, :] + nrm((DEPTH, H_MLSTM), 0.1)
    return {
        "x_prompt": nrm((BATCH, SEQ, D_MODEL)),
        "x_sample": nrm((DEC_BATCH, DEC_SEQ, D_MODEL)),
        "mem_prompt": nrm((BATCH, N_MEM, D_MODEL)),
        "cache_att_k": nrm((DEPTH, DEC_BATCH, att_rows, H_ATT, DH_ATT)),
        "cache_att_v": nrm((DEPTH, DEC_BATCH, att_rows, H_ATT, DH_ATT)),
        "cache_mem_k": nrm((DEPTH, DEC_BATCH, N_MEM, H_MEM, DH_MEM)),
        "cache_mem_v": nrm((DEPTH, DEC_BATCH, N_MEM, H_MEM, DH_MEM)),
        "state_mlstm_C": nrm((DEPTH, DEC_BATCH, H_MLSTM, DH_MLSTM, DH_MLSTM), 0.3),
        "state_mlstm_n": nrm((DEPTH, DEC_BATCH, H_MLSTM, DH_MLSTM), 0.3),
        "state_mlstm_m": nrm((DEPTH, DEC_BATCH, H_MLSTM), 0.5),
        "state_ffn_conv": nrm((DEPTH, DEC_BATCH, CONV_W - 1, 2 * D_FF)),
        "norm_mix_g": gain((DEPTH, D_MODEL)),
        "w_in": nrm((DEPTH, D_MODEL, N_IN), D_MODEL ** -0.5),
        "b_igate": nrm((DEPTH, H_MLSTM), 0.1),
        "b_fgate": b_f,
        "mlstm_head_g": gain((DEPTH, H_MLSTM, DH_MLSTM)),
        "rel_bias": nrm((DEPTH, H_ATT, N_REL), 0.1),
        "w_out": nrm((DEPTH, W_MIX, D_MODEL), W_MIX ** -0.5),
        "norm_cross_g": gain((DEPTH, D_MODEL)),
        "norm_mem_g": gain((DEPTH, D_MODEL)),
        "w_cq": nrm((DEPTH, D_MODEL, W_MEM), D_MODEL ** -0.5),
        "w_ckv": nrm((DEPTH, D_MODEL, 2 * W_MEM), D_MODEL ** -0.5),
        "w_co": nrm((DEPTH, W_MEM, D_MODEL), W_MEM ** -0.5),
        "norm_ffn_g": gain((DEPTH, D_MODEL)),
        "w_up": nrm((DEPTH, D_MODEL, 2 * D_FF), D_MODEL ** -0.5),
        "conv_w": nrm((DEPTH, CONV_W, 2 * D_FF), CONV_W ** -0.5),
        "conv_b": nrm((DEPTH, 2 * D_FF), 0.01),
        "w_down": nrm((DEPTH, D_FF, D_MODEL), D_FF ** -0.5),
        "norm_final_g": gain((D_MODEL,)),
    }


def reference(x_prompt, x_sample, mem_prompt, cache_att_k, cache_att_v, cache_mem_k, cache_mem_v,
              state_mlstm_C, state_mlstm_n, state_mlstm_m, state_ffn_conv,
              norm_mix_g, w_in, b_igate, b_fgate, mlstm_head_g, rel_bias, w_out,
              norm_cross_g, norm_mem_g, w_cq, w_ckv, w_co,
              norm_ffn_g, w_up, conv_w, conv_b, w_down, norm_final_g):
    f32 = jnp.float32
    xp, xs = x_prompt, x_sample
    Bp, T = xp.shape[:2]
    keep = min(BAND_PAST, T)
    pC, pn, pm, pk, pv, pmk, pmv, pconv = [], [], [], [], [], [], [], []
    sC, sn, sm, sk, sv, sconv = [], [], [], [], [], []
    for l in range(DEPTH):
        h = rmsnorm(xp, norm_mix_g[l])
        (qm, km, vm, ig, lf, o), (qa, ka, va) = mix_projections(h, w_in[l], b_igate[l], b_fgate[l])
        C0 = jnp.zeros((Bp, H_MLSTM, DH_MLSTM, DH_MLSTM), f32)
        n0 = jnp.zeros((Bp, H_MLSTM, DH_MLSTM), f32)
        m0 = jnp.zeros((Bp, H_MLSTM), f32)
        (C, n, m), hm = mlstm_scan(qm, km, vm, ig, lf, C0, n0, m0, CHUNK)
        ya = band_attention_prompt(qa, ka, va, rel_bias[l])
        xp = xp + jnp.concatenate([mlstm_out(hm, o, mlstm_head_g[l]), ya], axis=-1) @ w_out[l]
        mk, mv = mem_kv(mem_prompt, norm_mem_g[l], w_ckv[l])
        xp = xp + cross_attn(rmsnorm(xp, norm_cross_g[l]), mk, mv, w_cq[l], w_co[l])
        zc = jnp.zeros((Bp, CONV_W - 1, 2 * D_FF), xp.dtype)
        y, cst = conv_ffn(rmsnorm(xp, norm_ffn_g[l]), zc, w_up[l], conv_w[l], conv_b[l], w_down[l])
        xp = xp + y
        pC.append(C); pn.append(n); pm.append(m)
        pk.append(ka[:, T - keep:]); pv.append(va[:, T - keep:])
        pmk.append(mk); pmv.append(mv); pconv.append(cst)

        h = rmsnorm(xs, norm_mix_g[l])
        (qm, km, vm, ig, lf, o), (qa, ka, va) = mix_projections(h, w_in[l], b_igate[l], b_fgate[l])
        (C, n, m), hm = mlstm_scan(qm, km, vm, ig, lf, state_mlstm_C[l].astype(f32),
                                   state_mlstm_n[l].astype(f32), state_mlstm_m[l].astype(f32), xs.shape[1])
        ya = band_attention_sample(qa, ka, va, cache_att_k[l], cache_att_v[l], rel_bias[l])
        xs = xs + jnp.concatenate([mlstm_out(hm, o, mlstm_head_g[l]), ya], axis=-1) @ w_out[l]
        xs = xs + cross_attn(rmsnorm(xs, norm_cross_g[l]), cache_mem_k[l], cache_mem_v[l], w_cq[l], w_co[l])
        y, cst = conv_ffn(rmsnorm(xs, norm_ffn_g[l]), state_ffn_conv[l], w_up[l], conv_w[l], conv_b[l], w_down[l])
        xs = xs + y
        sC.append(C); sn.append(n); sm.append(m)
        sk.append(ka); sv.append(va); sconv.append(cst)

    y_prompt = rmsnorm(xp, norm_final_g)
    y_sample = rmsnorm(xs, norm_final_g)
    p_mlstm_C = jnp.stack(pC); p_mlstm_n = jnp.stack(pn); p_mlstm_m = jnp.stack(pm)
    p_att_k = jnp.stack(pk); p_att_v = jnp.stack(pv)
    p_mem_k = jnp.stack(pmk); p_mem_v = jnp.stack(pmv)
    p_ffn_conv = jnp.stack(pconv)
    s_mlstm_C = jnp.stack(sC); s_mlstm_n = jnp.stack(sn); s_mlstm_m = jnp.stack(sm)
    s_att_k = jnp.stack(sk); s_att_v = jnp.stack(sv)
    s_ffn_conv = jnp.stack(sconv)
    return (y_prompt, y_sample, p_mlstm_C, p_mlstm_n, p_mlstm_m, p_att_k, p_att_v, p_mem_k, p_mem_v,
            p_ffn_conv, s_mlstm_C, s_mlstm_n, s_mlstm_m, s_att_k, s_att_v, s_ffn_conv)
```

```python
import functools

import jax
import jax.numpy as jnp
from jax import lax
from jax.experimental import pallas as pl
from jax.experimental.pallas import tpu as pltpu

F32 = jnp.float32
BF16 = jnp.bfloat16

D_MODEL = 2048
CHUNK = 64
H_MLSTM = 4
DH_MLSTM = 256
W_MLSTM = H_MLSTM * DH_MLSTM
H_ATT = 8
DH_ATT = 128
W_ATT = H_ATT * DH_ATT
BAND_CHUNKS = 8
BAND_PAST = BAND_CHUNKS * CHUNK
BAND_LEN = BAND_PAST + CHUNK
REL_CLIP = 256
N_MEM = 256
H_MEM = 4
DH_MEM = 128
W_MEM = H_MEM * DH_MEM
D_FF = 5632
CONV_W = 3
EPS = 1e-6
NEG = -1e30
N_GATE = 2 * H_MLSTM
N_MAIN = 3 * W_MLSTM + W_MLSTM + 3 * W_ATT
ZC_QM, ZC_KM, ZC_VM, ZC_OG, ZC_QA, ZC_KA, ZC_VA = range(7)

LANES = 128
SUBLANES = 8
VMEM_LIMIT = 56 * 1024 * 1024

NT = (((1,), (1,)), ((), ()))
TN = (((0,), (0,)), ((), ()))


def _params(sem):
    return pltpu.CompilerParams(dimension_semantics=sem, vmem_limit_bytes=VMEM_LIMIT)


def _rms(x, g):
    ms = jnp.mean(x * x, axis=-1, keepdims=True)
    return x * lax.rsqrt(ms + EPS) * g


def _sigmoid(x):
    return 1.0 / (1.0 + jnp.exp(-x))


def _log_sigmoid(x):
    return jnp.minimum(x, 0.0) - jnp.log1p(jnp.exp(-jnp.abs(x)))


def _dot(a, b):
    return jnp.dot(a, b, preferred_element_type=F32)


def _inproj_kernel(x_ref, g_ref, w_ref, wg_ref, wgt_ref, z_ref, gc_ref, gr_ref, h_scr):
    @pl.when(pl.program_id(1) == 0)
    def _():
        h = _rms(x_ref[...], g_ref[...]).astype(BF16)
        h_scr[...] = h
        gc_ref[...] = _dot(h, wg_ref[...])
        gr_ref[...] = lax.dot_general(wgt_ref[...], h, NT, preferred_element_type=F32)

    z_ref[...] = _dot(h_scr[...], w_ref[...])


def _inproj(x, g, w_main, wg, wgt, tm, tn):
    r = x.shape[0]
    n = w_main.shape[1]
    return pl.pallas_call(
        _inproj_kernel,
        grid=(r // tm, n // tn),
        in_specs=[
            pl.BlockSpec((tm, D_MODEL), lambda i, j: (i, 0)),
            pl.BlockSpec((1, D_MODEL), lambda i, j: (0, 0)),
            pl.BlockSpec((D_MODEL, tn), lambda i, j: (0, j)),
            pl.BlockSpec((D_MODEL, LANES), lambda i, j: (0, 0)),
            pl.BlockSpec((2 * SUBLANES, D_MODEL), lambda i, j: (0, 0)),
        ],
        out_specs=[
            pl.BlockSpec((tm, tn), lambda i, j: (i, j)),
            pl.BlockSpec((tm, LANES), lambda i, j: (i, 0)),
            pl.BlockSpec((2 * SUBLANES, tm), lambda i, j: (0, i)),
        ],
        out_shape=[
            jax.ShapeDtypeStruct((r, n), F32),
            jax.ShapeDtypeStruct((r, LANES), F32),
            jax.ShapeDtypeStruct((2 * SUBLANES, r), F32),
        ],
        scratch_shapes=[pltpu.VMEM((tm, D_MODEL), BF16)],
        compiler_params=_params(("parallel", "arbitrary")),
        name="inproj",
    )(x, g, w_main, wg, wgt)


def _norm_mm_kernel(x_ref, g_ref, w_ref, o_ref):
    h = _rms(x_ref[...], g_ref[...]).astype(BF16)
    o_ref[...] = _dot(h, w_ref[...])


def _norm_mm(x, g, w, tn):
    r, n = x.shape[0], w.shape[1]
    return pl.pallas_call(
        _norm_mm_kernel,
        grid=(n // tn,),
        in_specs=[
            pl.BlockSpec((r, D_MODEL), lambda j: (0, 0)),
            pl.BlockSpec((1, D_MODEL), lambda j: (0, 0)),
            pl.BlockSpec((D_MODEL, tn), lambda j: (0, j)),
        ],
        out_specs=pl.BlockSpec((r, tn), lambda j: (0, j)),
        out_shape=jax.ShapeDtypeStruct((r, n), F32),
        compiler_params=_params(("parallel",)),
        name="mem_kv",
    )(x, g, w)


def _mlstm_kernel(q_ref, k_ref, v_ref, og_ref, gc_ref, gr_ref, brow_ref, bcol_ref, hg_ref,
                  c0_ref, n0_ref, m0_ref, y_ref, c_out, n_out, m_out, c_scr, n_scr, m_scr):
    c = pl.program_id(1)

    @pl.when(c == 0)
    def _():
        c_scr[...] = c0_ref[0]
        n_scr[...] = n0_ref[0]
        m_scr[...] = m0_ref[0]

    g_col = gc_ref[...] + brow_ref[...]
    g_row = gr_ref[0] + bcol_ref[...]
    lf_col = _log_sigmoid(g_col)
    lf_row = _log_sigmoid(g_row)
    ti = lax.broadcasted_iota(jnp.int32, (CHUNK, CHUNK), 0)
    si = lax.broadcasted_iota(jnp.int32, (CHUNK, CHUNK), 1)
    causal = si <= ti
    tri = causal.astype(F32)
    tri_t = (ti <= si).astype(F32)
    b_col_all = jnp.dot(tri, lf_col, preferred_element_type=F32, precision=lax.Precision.HIGHEST)
    b_row_all = jnp.dot(lf_row, tri_t, preferred_element_type=F32, precision=lax.Precision.HIGHEST)

    for h in range(H_MLSTM):
        sl = slice(h * DH_MLSTM, (h + 1) * DH_MLSTM)
        qf = q_ref[:, sl]
        kf = k_ref[:, sl] * (DH_MLSTM ** -0.5)
        vf = v_ref[:, sl]
        qb, kb, vb = qf.astype(BF16), kf.astype(BF16), vf.astype(BF16)
        b_c = b_col_all[:, H_MLSTM + h:H_MLSTM + h + 1]
        b_r = b_row_all[H_MLSTM + h:H_MLSTM + h + 1, :]
        ig_c = g_col[:, h:h + 1]
        ig_r = g_row[h:h + 1, :]
        m = m_scr[h:h + 1, 0:1]

        dm = jnp.where(causal, b_c - b_r + ig_r, NEG)
        inter = b_c + m
        m_t = jnp.maximum(inter, jnp.max(dm, axis=-1, keepdims=True))
        w_inter = jnp.exp(inter - m_t)
        s = lax.dot_general(qb, kb, NT, preferred_element_type=F32)
        p = jnp.exp(dm - m_t) * s
        c_prev = c_scr[h]
        n_prev = n_scr[h:h + 1, :]
        num = w_inter * lax.dot_general(qb, c_prev.astype(BF16), NT, preferred_element_type=F32)
        num = num + _dot(p.astype(BF16), vb)
        den = w_inter * jnp.sum(qf * n_prev, axis=-1, keepdims=True) + jnp.sum(p, axis=-1, keepdims=True)
        hh = num / jnp.maximum(jnp.abs(den), jnp.exp(-m_t))

        hn = hh * lax.rsqrt(jnp.mean(hh * hh, axis=-1, keepdims=True) + EPS) * hg_ref[h:h + 1, :]
        y_ref[:, sl] = (_sigmoid(og_ref[:, sl]) * hn).astype(y_ref.dtype)

        b_last = b_c[CHUNK - 1:CHUNK, :]
        g_r = b_last - b_r + ig_r
        g_c = b_last - b_c + ig_c
        m_new = jnp.maximum(b_last + m, jnp.max(g_r, axis=-1, keepdims=True))
        a_old = jnp.exp(b_last + m - m_new)
        a_c = jnp.exp(g_c - m_new)
        av = (a_c * vf).astype(BF16)
        c_scr[h] = a_old * c_prev + lax.dot_general(av, kb, TN, preferred_element_type=F32)
        n_scr[h:h + 1, :] = a_old * n_prev + jnp.sum(a_c * kf, axis=0, keepdims=True)
        m_scr[h:h + 1, :] = jnp.broadcast_to(m_new, (1, LANES))

    @pl.when(c == pl.num_programs(1) - 1)
    def _():
        c_out[0] = c_scr[...]
        n_out[0] = n_scr[...]
        m_out[0] = m_scr[...]


def _mlstm(z, gc, gr3, brow, bcol, head_g, c0, n0, m0b, n_streams, n_chunks):
    r = z.shape[0]

    def zspec(col):
        return pl.BlockSpec((CHUNK, W_MLSTM), lambda s, c, col=col: (s * n_chunks + c, col))

    def state_spec(shape):
        nd = len(shape)
        return pl.BlockSpec((1,) + shape, lambda s, c: (s,) + (0,) * nd)

    return pl.pallas_call(
        _mlstm_kernel,
        grid=(n_streams, n_chunks),
        in_specs=[
            zspec(ZC_QM), zspec(ZC_KM), zspec(ZC_VM), zspec(ZC_OG),
            pl.BlockSpec((CHUNK, LANES), lambda s, c: (s * n_chunks + c, 0)),
            pl.BlockSpec((1, 2 * SUBLANES, CHUNK), lambda s, c: (s * n_chunks + c, 0, 0)),
            pl.BlockSpec((1, LANES), lambda s, c: (0, 0)),
            pl.BlockSpec((2 * SUBLANES, 1), lambda s, c: (0, 0)),
            pl.BlockSpec((H_MLSTM, DH_MLSTM), lambda s, c: (0, 0)),
            state_spec((H_MLSTM, DH_MLSTM, DH_MLSTM)),
            state_spec((H_MLSTM, DH_MLSTM)),
            state_spec((SUBLANES, LANES)),
        ],
        out_specs=[
            pl.BlockSpec((CHUNK, W_MLSTM), lambda s, c: (s * n_chunks + c, 0)),
            state_spec((H_MLSTM, DH_MLSTM, DH_MLSTM)),
            state_spec((H_MLSTM, DH_MLSTM)),
            state_spec((SUBLANES, LANES)),
        ],
        out_shape=[
            jax.ShapeDtypeStruct((r, W_MLSTM), BF16),
            jax.ShapeDtypeStruct((n_streams, H_MLSTM, DH_MLSTM, DH_MLSTM), F32),
            jax.ShapeDtypeStruct((n_streams, H_MLSTM, DH_MLSTM), F32),
            jax.ShapeDtypeStruct((n_streams, SUBLANES, LANES), F32),
        ],
        scratch_shapes=[
            pltpu.VMEM((H_MLSTM, DH_MLSTM, DH_MLSTM), F32),
            pltpu.VMEM((H_MLSTM, DH_MLSTM), F32),
            pltpu.VMEM((SUBLANES, LANES), F32),
        ],
        compiler_params=_params(("parallel", "arbitrary")),
        name="mlstm_scan",
    )(z, z, z, z, gc, gr3, brow, bcol, head_g, c0, n0, m0b)


def _band_attn_kernel(q_ref, kp_ref, kc_ref, vp_ref, vc_ref, bias_ref, o_ref, kwin, vwin,
                      *, n_sub, mask_first):
    i = pl.program_id(0)
    rows_cur = n_sub * CHUNK
    kwin[0:BAND_PAST, :] = kp_ref[...].astype(BF16)
    kwin[BAND_PAST:BAND_PAST + rows_cur, :] = kc_ref[...].astype(BF16)
    vwin[0:BAND_PAST, :] = vp_ref[...].astype(BF16)
    vwin[BAND_PAST:BAND_PAST + rows_cur, :] = vc_ref[...].astype(BF16)
    scale = DH_ATT ** -0.5
    kk = lax.broadcasted_iota(jnp.int32, (CHUNK, BAND_LEN), 1)

    def body(j, carry):
        r0 = pl.multiple_of(j * CHUNK, CHUNK)
        if mask_first:
            first_valid = jnp.where(i == 0, BAND_PAST - j * CHUNK, 0)
            valid = kk >= first_valid
        for h in range(H_ATT):
            hs = slice(h * DH_ATT, (h + 1) * DH_ATT)
            qh = q_ref[pl.ds(r0, CHUNK), hs].astype(BF16)
            kh = kwin[pl.ds(r0, BAND_LEN), hs]
            vh = vwin[pl.ds(r0, BAND_LEN), hs]
            s = lax.dot_general(qh, kh, NT, preferred_element_type=F32) * scale + bias_ref[h]
            if mask_first:
                s = jnp.where(valid, s, NEG)
            e = jnp.exp(s - jnp.max(s, axis=-1, keepdims=True))
            l = jnp.sum(e, axis=-1, keepdims=True)
            o = _dot(e.astype(BF16), vh) / l
            o_ref[pl.ds(r0, CHUNK), hs] = o.astype(o_ref.dtype)
        return carry

    lax.fori_loop(0, n_sub, body, 0)


def _band_attn(z, k_prev, v_prev, bias, n_tiles, n_sub, prev_from_z):
    r = z.shape[0]
    rows = n_sub * CHUNK
    if prev_from_z:
        kp_spec = pl.BlockSpec((BAND_PAST, W_ATT), lambda i: (jnp.maximum(i - 1, 0), ZC_KA))
        vp_spec = pl.BlockSpec((BAND_PAST, W_ATT), lambda i: (jnp.maximum(i - 1, 0), ZC_VA))
    else:
        kp_spec = pl.BlockSpec((BAND_PAST, W_ATT), lambda i: (i, 0))
        vp_spec = pl.BlockSpec((BAND_PAST, W_ATT), lambda i: (i, 0))
    return pl.pallas_call(
        functools.partial(_band_attn_kernel, n_sub=n_sub, mask_first=prev_from_z),
        grid=(n_tiles,),
        in_specs=[
            pl.BlockSpec((rows, W_ATT), lambda i: (i, ZC_QA)),
            kp_spec,
            pl.BlockSpec((rows, W_ATT), lambda i: (i, ZC_KA)),
            vp_spec,
            pl.BlockSpec((rows, W_ATT), lambda i: (i, ZC_VA)),
            pl.BlockSpec((H_ATT, CHUNK, BAND_LEN), lambda i: (0, 0, 0)),
        ],
        out_specs=pl.BlockSpec((rows, W_ATT), lambda i: (i, 0)),
        out_shape=jax.ShapeDtypeStruct((r, W_ATT), BF16),
        scratch_shapes=[
            pltpu.VMEM((BAND_PAST + rows, W_ATT), BF16),
            pltpu.VMEM((BAND_PAST + rows, W_ATT), BF16),
        ],
        compiler_params=_params(("parallel",)),
        name="band_attn",
    )(z, k_prev, z, v_prev, z, bias)


def _mix_out_kernel(x_ref, ym_ref, ya_ref, wo_ref, g_ref, wcq_ref, mk_ref, mv_ref, wco_ref,
                    o_ref, att_scr, *, n_streams):
    tm = x_ref.shape[0]
    rps = tm // n_streams
    y = _dot(ym_ref[...], wo_ref[0:W_MLSTM, :]) + _dot(ya_ref[...], wo_ref[W_MLSTM:W_MLSTM + W_ATT, :])
    x1 = x_ref[...] + y
    hc = _rms(x1, g_ref[...]).astype(BF16)
    q = _dot(hc, wcq_ref[...])
    scale = DH_MEM ** -0.5
    for s in range(n_streams):
        rs = slice(s * rps, (s + 1) * rps)
        ms = slice(s * N_MEM, (s + 1) * N_MEM)
        for h in range(H_MEM):
            hs = slice(h * DH_MEM, (h + 1) * DH_MEM)
            qh = q[rs, hs].astype(BF16)
            kh = mk_ref[ms, hs].astype(BF16)
            vh = mv_ref[ms, hs].astype(BF16)
            sc = lax.dot_general(qh, kh, NT, preferred_element_type=F32) * scale
            e = jnp.exp(sc - jnp.max(sc, axis=-1, keepdims=True))
            l = jnp.sum(e, axis=-1, keepdims=True)
            att_scr[rs, hs] = (_dot(e.astype(BF16), vh) / l).astype(BF16)
    o_ref[...] = x1 + _dot(att_scr[...], wco_ref[...])


def _mix_out(x, ym, ya, w_out, g, w_cq, mk, mv, mcol_k, mcol_v, w_co, tm, n_streams):
    r = x.shape[0]
    nt = r // tm
    shared_mem = mk.shape[0] == n_streams * N_MEM

    def mem_spec(col):
        if shared_mem:
            return pl.BlockSpec((n_streams * N_MEM, W_MEM), lambda i: (0, col))
        return pl.BlockSpec((n_streams * N_MEM, W_MEM), lambda i: (i, col))

    const = lambda i: (0, 0)
    return pl.pallas_call(
        functools.partial(_mix_out_kernel, n_streams=n_streams),
        grid=(nt,),
        in_specs=[
            pl.BlockSpec((tm, D_MODEL), lambda i: (i, 0)),
            pl.BlockSpec((tm, W_MLSTM), lambda i: (i, 0)),
            pl.BlockSpec((tm, W_ATT), lambda i: (i, 0)),
            pl.BlockSpec((W_MLSTM + W_ATT, D_MODEL), const, pipeline_mode=pl.Buffered(1)),
            pl.BlockSpec((1, D_MODEL), const),
            pl.BlockSpec((D_MODEL, W_MEM), const, pipeline_mode=pl.Buffered(1)),
            mem_spec(mcol_k),
            mem_spec(mcol_v),
            pl.BlockSpec((W_MEM, D_MODEL), const, pipeline_mode=pl.Buffered(1)),
        ],
        out_specs=pl.BlockSpec((tm, D_MODEL), lambda i: (i, 0)),
        out_shape=jax.ShapeDtypeStruct((r, D_MODEL), F32),
        scratch_shapes=[pltpu.VMEM((tm, W_MEM), BF16)],
        compiler_params=_params(("parallel",)),
        name="mix_out_cross",
    )(x, ym, ya, w_out, g, w_cq, mk, mv, w_co)


def _ffn_kernel(x_ref, g_ref, wa_ref, wb_ref, cwa_ref, cwb_ref, cba_ref, cbb_ref, sta_ref, stb_ref,
                wd_ref, o_ref, outa_ref, outb_ref, h_scr, ua_scr, ub_scr, gate_scr, *, n_streams):
    i = pl.program_id(0)
    f = pl.program_id(1)
    tm = x_ref.shape[0]
    tf = wa_ref.shape[1]
    rps = tm // n_streams
    halo = CONV_W - 1
    cols = pl.ds(pl.multiple_of(f * tf, tf), tf)

    @pl.when(f == 0)
    def _():
        h_scr[...] = _rms(x_ref[...], g_ref[...]).astype(BF16)
        o_ref[...] = x_ref[...]

    @pl.when(i == 0)
    def _():
        outa_ref[:, :, cols] = sta_ref[...]
        outb_ref[:, :, cols] = stb_ref[...]

    h = h_scr[...]
    for u_scr, w_ref, cw_ref, cb_ref, out_ref, is_gate in (
            (ua_scr, wa_ref, cwa_ref, cba_ref, outa_ref, True),
            (ub_scr, wb_ref, cwb_ref, cbb_ref, outb_ref, False)):
        u = _dot(h, w_ref[...])
        for s in range(n_streams):
            u_scr[s, SUBLANES:SUBLANES + rps, :] = u[s * rps:(s + 1) * rps, :]
            u_scr[s, SUBLANES - halo:SUBLANES, :] = out_ref[s, :, cols]
        for s in range(n_streams):
            cv = cb_ref[...]
            for tap in range(CONV_W):
                lo = SUBLANES - halo + tap
                cv = cv + cw_ref[tap:tap + 1, :] * u_scr[s, lo:lo + rps, :]
            out_ref[s, :, cols] = u_scr[s, SUBLANES + rps - halo:SUBLANES + rps, :]
            rs = slice(s * rps, (s + 1) * rps)
            if is_gate:
                gate_scr[rs, :] = cv * _sigmoid(cv)
            else:
                gate_scr[rs, :] = gate_scr[rs, :] * cv

    o_ref[...] += _dot(gate_scr[...].astype(BF16), wd_ref[...])


def _ffn(x, g, w_up, conv_w, conv_b, w_down, st_a, st_b, tm, tf, n_streams):
    r = x.shape[0]
    ni, nf = r // tm, D_FF // tf
    rps = tm // n_streams
    halo = CONV_W - 1
    st_spec = pl.BlockSpec((n_streams, halo, tf), lambda i, f: (0, 0, f))
    out_st_spec = pl.BlockSpec((n_streams, halo, D_FF), lambda i, f: (0, 0, 0))
    return pl.pallas_call(
        functools.partial(_ffn_kernel, n_streams=n_streams),
        grid=(ni, nf),
        in_specs=[
            pl.BlockSpec((tm, D_MODEL), lambda i, f: (i, 0)),
            pl.BlockSpec((1, D_MODEL), lambda i, f: (0, 0)),
            pl.BlockSpec((D_MODEL, tf), lambda i, f: (0, f)),
            pl.BlockSpec((D_MODEL, tf), lambda i, f: (0, nf + f)),
            pl.BlockSpec((CONV_W, tf), lambda i, f: (0, f)),
            pl.BlockSpec((CONV_W, tf), lambda i, f: (0, nf + f)),
            pl.BlockSpec((1, tf), lambda i, f: (0, f)),
            pl.BlockSpec((1, tf), lambda i, f: (0, nf + f)),
            st_spec, st_spec,
            pl.BlockSpec((tf, D_MODEL), lambda i, f: (f, 0)),
        ],
        out_specs=[
            pl.BlockSpec((tm, D_MODEL), lambda i, f: (i, 0)),
            out_st_spec, out_st_spec,
        ],
        out_shape=[
            jax.ShapeDtypeStruct((r, D_MODEL), F32),
            jax.ShapeDtypeStruct((n_streams, halo, D_FF), F32),
            jax.ShapeDtypeStruct((n_streams, halo, D_FF), F32),
        ],
        scratch_shapes=[
            pltpu.VMEM((tm, D_MODEL), BF16),
            pltpu.VMEM((n_streams, SUBLANES + rps, tf), F32),
            pltpu.VMEM((n_streams, SUBLANES + rps, tf), F32),
            pltpu.VMEM((tm, tf), F32),
        ],
        compiler_params=_params(("arbitrary", "arbitrary")),
        name="conv_ffn",
    )(x, g, w_up, w_up, conv_w, conv_w, conv_b, conv_b, st_a, st_b, w_down)


def _final_norm_kernel(x_ref, g_ref, o_ref):
    o_ref[...] = _rms(x_ref[...], g_ref[...])


def _final_norm(x, g, tm):
    r = x.shape[0]
    return pl.pallas_call(
        _final_norm_kernel,
        grid=(r // tm,),
        in_specs=[pl.BlockSpec((tm, D_MODEL), lambda i: (i, 0)),
                  pl.BlockSpec((1, D_MODEL), lambda i: (0, 0))],
        out_specs=pl.BlockSpec((tm, D_MODEL), lambda i: (i, 0)),
        out_shape=jax.ShapeDtypeStruct((r, D_MODEL), F32),
        compiler_params=_params(("parallel",)),
        name="final_norm",
    )(x, g)


def _layer_group(x, wts, n_streams, seq, mlstm_state, att_cache, mem_kv, conv_state, tiles):
    n_chunks = seq // CHUNK
    z, gc, gr = _inproj(x, wts["g_mix"], wts["w_main"], wts["w_gate"], wts["w_gate_t"],
                        tiles["tm_in"], tiles["tn_in"])
    r = x.shape[0]
    gr3 = gr.reshape(2 * SUBLANES, r // CHUNK, CHUNK).transpose(1, 0, 2)

    c0, n0, m0 = mlstm_state
    m0b = jnp.zeros((n_streams, SUBLANES, LANES), F32)
    m0b = m0b.at[:, :H_MLSTM, :].set(jnp.broadcast_to(m0[:, :, None], (n_streams, H_MLSTM, LANES)))
    ym, c_new, n_new, m_newb = _mlstm(z, gc, gr3, wts["gate_b_row"], wts["gate_b_col"], wts["head_g"],
                                      c0, n0, m0b, n_streams, n_chunks)
    m_new = m_newb[:, :H_MLSTM, 0]

    if att_cache is None:
        ya = _band_attn(z, z, z, wts["att_bias"], r // BAND_PAST, BAND_CHUNKS, True)
    else:
        ya = _band_attn(z, att_cache[0], att_cache[1], wts["att_bias"], n_streams, 1, False)

    mk, mv, mcol_k, mcol_v = mem_kv
    x = _mix_out(x, ym, ya, wts["w_out"], wts["g_cross"], wts["w_cq"], mk, mv, mcol_k, mcol_v,
                 wts["w_co"], tiles["tm_mix"], tiles["mix_streams"])

    st_a, st_b = conv_state
    x, cs_a, cs_b = _ffn(x, wts["g_ffn"], wts["w_up"], wts["conv_w"], wts["conv_b"], wts["w_down"],
                         st_a, st_b, tiles["tm_ffn"], tiles["tf"], tiles["ffn_streams"])
    conv_new = jnp.concatenate([cs_a, cs_b], axis=-1)
    return x, z, (c_new, n_new, m_new), conv_new


def _layer_weights(l, norm_mix_g, w_in, b_igate, b_fgate, mlstm_head_g, rel_bias, w_out,
                   norm_cross_g, norm_mem_g, w_cq, w_ckv, w_co, norm_ffn_g, w_up, conv_w, conv_b, w_down):
    g0 = 3 * W_MLSTM
    wi = w_in[l]
    w_gate = wi[:, g0:g0 + N_GATE]
    gate_b = jnp.concatenate([b_igate[l], b_fgate[l]])
    rel = BAND_PAST + jnp.arange(CHUNK)[:, None] - jnp.arange(BAND_LEN)[None, :]
    att_bias = rel_bias[l][:, jnp.clip(rel, -REL_CLIP, REL_CLIP) + REL_CLIP].astype(F32)
    return {
        "g_mix": norm_mix_g[l][None, :],
        "w_main": jnp.concatenate([wi[:, :g0], wi[:, g0 + N_GATE:]], axis=1).astype(BF16),
        "w_gate": jnp.pad(w_gate, ((0, 0), (0, LANES - N_GATE))).astype(BF16),
        "w_gate_t": jnp.pad(w_gate.T, ((0, 2 * SUBLANES - N_GATE), (0, 0))).astype(BF16),
        "gate_b_row": jnp.pad(gate_b, (0, LANES - N_GATE))[None, :],
        "gate_b_col": jnp.pad(gate_b, (0, 2 * SUBLANES - N_GATE))[:, None],
        "head_g": mlstm_head_g[l],
        "att_bias": att_bias,
        "w_out": w_out[l].astype(BF16),
        "g_cross": norm_cross_g[l][None, :],
        "g_mem": norm_mem_g[l][None, :],
        "w_cq": w_cq[l].astype(BF16),
        "w_ckv": w_ckv[l].astype(BF16),
        "w_co": w_co[l].astype(BF16),
        "g_ffn": norm_ffn_g[l][None, :],
        "w_up": w_up[l].astype(BF16),
        "conv_w": conv_w[l],
        "conv_b": conv_b[l][None, :],
        "w_down": w_down[l].astype(BF16),
    }


PROMPT_TILES = dict(tm_in=1024, tn_in=512, tm_mix=512, mix_streams=1, tm_ffn=512, tf=512, ffn_streams=1)
SAMPLE_TILES = dict(tm_in=512, tn_in=512, tm_mix=512, mix_streams=8, tm_ffn=512, tf=512, ffn_streams=8)


def kernel(x_prompt, x_sample, mem_prompt, cache_att_k, cache_att_v, cache_mem_k, cache_mem_v,
           state_mlstm_C, state_mlstm_n, state_mlstm_m, state_ffn_conv,
           norm_mix_g, w_in, b_igate, b_fgate, mlstm_head_g, rel_bias, w_out,
           norm_cross_g, norm_mem_g, w_cq, w_ckv, w_co,
           norm_ffn_g, w_up, conv_w, conv_b, w_down, norm_final_g):
    depth = w_in.shape[0]
    bp, t = x_prompt.shape[:2]
    bs, ts = x_sample.shape[:2]
    keep = min(BAND_PAST, t)
    assert bp == 1 and t % BAND_PAST == 0 and ts == CHUNK and cache_att_k.shape[2] == BAND_PAST
    assert SAMPLE_TILES["mix_streams"] == bs and SAMPLE_TILES["ffn_streams"] == bs

    xp = x_prompt.reshape(bp * t, D_MODEL)
    xs = x_sample.reshape(bs * ts, D_MODEL)
    mem = mem_prompt.reshape(bp * N_MEM, D_MODEL)
    outs = {k: [] for k in ("pC", "pn", "pm", "pk", "pv", "pmk", "pmv", "pconv",
                            "sC", "sn", "sm", "sk", "sv", "sconv")}
    for l in range(depth):
        wts = _layer_weights(l, norm_mix_g, w_in, b_igate, b_fgate, mlstm_head_g, rel_bias, w_out,
                             norm_cross_g, norm_mem_g, w_cq, w_ckv, w_co, norm_ffn_g, w_up, conv_w,
                             conv_b, w_down)
        mkv = _norm_mm(mem, wts["g_mem"], wts["w_ckv"], W_MEM)
        zero_state = (jnp.zeros((bp, H_MLSTM, DH_MLSTM, DH_MLSTM), F32),
                      jnp.zeros((bp, H_MLSTM, DH_MLSTM), F32),
                      jnp.zeros((bp, H_MLSTM), F32))
        zero_conv = jnp.zeros((bp, CONV_W - 1, D_FF), F32)
        xp, zp, (c, n, m), conv = _layer_group(
            xp, wts, bp, t, zero_state, None, (mkv, mkv, 0, 1), (zero_conv, zero_conv), PROMPT_TILES)
        outs["pC"].append(c); outs["pn"].append(n); outs["pm"].append(m)
        ka = zp[:, ZC_KA * W_ATT:(ZC_KA + 1) * W_ATT].reshape(bp, t, H_ATT, DH_ATT)
        va = zp[:, ZC_VA * W_ATT:(ZC_VA + 1) * W_ATT].reshape(bp, t, H_ATT, DH_ATT)
        outs["pk"].append(ka[:, t - keep:]); outs["pv"].append(va[:, t - keep:])
        outs["pmk"].append(mkv[:, :W_MEM].reshape(bp, N_MEM, H_MEM, DH_MEM))
        outs["pmv"].append(mkv[:, W_MEM:].reshape(bp, N_MEM, H_MEM, DH_MEM))
        outs["pconv"].append(conv)

        state = (state_mlstm_C[l], state_mlstm_n[l], state_mlstm_m[l])
        cache = (cache_att_k[l].reshape(bs * BAND_PAST, W_ATT), cache_att_v[l].reshape(bs * BAND_PAST, W_ATT))
        memk = cache_mem_k[l].reshape(bs * N_MEM, W_MEM)
        memv = cache_mem_v[l].reshape(bs * N_MEM, W_MEM)
        conv_in = (state_ffn_conv[l][:, :, :D_FF], state_ffn_conv[l][:, :, D_FF:])
        xs, zs, (c, n, m), conv = _layer_group(
            xs, wts, bs, ts, state, cache, (memk, memv, 0, 0), conv_in, SAMPLE_TILES)
        outs["sC"].append(c); outs["sn"].append(n); outs["sm"].append(m)
        outs["sk"].append(zs[:, ZC_KA * W_ATT:(ZC_KA + 1) * W_ATT].reshape(bs, ts, H_ATT, DH_ATT))
        outs["sv"].append(zs[:, ZC_VA * W_ATT:(ZC_VA + 1) * W_ATT].reshape(bs, ts, H_ATT, DH_ATT))
        outs["sconv"].append(conv)

    gfin = norm_final_g[None, :]
    y_prompt = _final_norm(xp, gfin, 512).reshape(bp, t, D_MODEL)
    y_sample = _final_norm(xs, gfin, 512).reshape(bs, ts, D_MODEL)
    st = {k: jnp.stack(v) for k, v in outs.items()}
    return (y_prompt, y_sample, st["pC"], st["pn"], st["pm"], st["pk"], st["pv"], st["pmk"], st["pmv"],
            st["pconv"], st["sC"], st["sn"], st["sm"], st["sk"], st["sv"], st["sconv"])
```
